```python
import math
import jax, jax.numpy as jnp
from jax import lax
import numpy as np

D_MODEL = 1024
BATCH = 8
SEQ = 2048
DEPTH = 4
DEC_BATCH = 128
DEC_SEQ = 8
PAST_LEN = 16384
PAGE_SIZE = 128

D_RNN = D_MODEL
D_CONV = D_MODEL
N_BLOCKS = 8
BLOCK = D_RNN // N_BLOCKS
CONV4_W = 4
CONV3_W = 3
C_RG = 8.0
ALPHA = (2.0 * DEPTH) ** 0.25
BETA = (8.0 * DEPTH) ** -0.25
LN_EPS = 1e-5
SPLITS = np.cumsum([D_RNN, D_RNN, D_CONV, D_CONV, D_CONV, D_CONV, D_MODEL]).tolist()
N_IN = 2 * D_RNN + 4 * D_CONV + 2 * D_MODEL

kernel_name = "hawk_shortconv_gated_parallel_deepnorm_step"


def _causal_dwconv(buf, u, w):
    width = w.shape[0]
    T = u.shape[1]
    up = jnp.concatenate([buf.astype(u.dtype), u], axis=1)
    y = up[:, 0:T] * w[0]
    for k in range(1, width):
        y = y + up[:, k:k + T] * w[k]
    return y, up[:, T:]


def _rglru(xr, h0, reset, w_a, b_a, w_x, b_x, lam):
    bsz, T, _ = xr.shape
    xb = xr.reshape(bsz, T, N_BLOCKS, BLOCK)
    ga = jnp.einsum('btni,nij->btnj', xb, w_a).reshape(bsz, T, D_RNN) + b_a
    gx = jnp.einsum('btni,nij->btnj', xb, w_x).reshape(bsz, T, D_RNN) + b_x
    r = jax.nn.sigmoid(ga.astype(jnp.float32))
    i = jax.nn.sigmoid(gx.astype(jnp.float32))
    log_a = -C_RG * r * jax.nn.softplus(-lam.astype(jnp.float32))
    a = jnp.exp(log_a)
    mult = jnp.sqrt(-jnp.expm1(2.0 * log_a))
    mult = jnp.where(reset[None, :, None], 1.0, mult)
    b = mult * i * xr.astype(jnp.float32)

    def step(h, ab):
        a_t, b_t = ab
        h = a_t * h + b_t
        return h, h

    hT, hs = lax.scan(step, h0.astype(jnp.float32), (a.swapaxes(0, 1), b.swapaxes(0, 1)))
    return hs.swapaxes(0, 1).astype(xr.dtype), hT.astype(xr.dtype)


def _layernorm(x, g, b):
    xf = x.astype(jnp.float32)
    mu = jnp.mean(xf, axis=-1, keepdims=True)
    var = jnp.mean(jnp.square(xf - mu), axis=-1, keepdims=True)
    y = (xf - mu) * lax.rsqrt(var + LN_EPS)
    return (y * g.astype(jnp.float32) + b.astype(jnp.float32)).astype(x.dtype)


def _layer(x, h0, c4_buf, c3_buf, reset, w_in, b_in, conv4_w, conv4_b, w_rg_a, b_rg_a,
           w_rg_x, b_rg_x, rg_lambda, conv3_w, w_rnn_out, w_conv_out, w_out, ln_g, ln_b):
    z = x @ w_in + b_in
    xr, gr, cb, cc, ch, gc, g_rnn, g_conv = jnp.split(z, SPLITS, axis=-1)
    xr_c, c4_new = _causal_dwconv(c4_buf, xr, conv4_w)
    xr_c = xr_c + conv4_b
    hs, hT = _rglru(xr_c, h0, reset, w_rg_a, b_rg_a, w_rg_x, b_rg_x, rg_lambda)
    y_rnn = (hs * jax.nn.silu(gr)) @ w_rnn_out
    v, c3_new = _causal_dwconv(c3_buf, cc * ch, conv3_w)
    y_conv = (cb * v * jax.nn.silu(gc)) @ w_conv_out
    m = jax.nn.sigmoid(g_rnn) * y_rnn + jax.nn.sigmoid(g_conv) * y_conv
    out = m @ w_out
    x = _layernorm(ALPHA * x + out, ln_g, ln_b)
    return x, hT, c4_new, c3_new


def setup_inputs(seed: int = 0) -> dict:
    key = jax.random.key(seed)
    ks = jax.random.split(key, 24)
    f32 = jnp.float32
    n = lambda k, s, sc: jax.random.normal(k, s, f32) * sc
    u = jax.random.uniform(ks[13], (DEPTH, D_RNN), f32, 0.9, 0.999)
    a_base = u ** (1.0 / C_RG)
    rg_lambda = jnp.log(a_base) - jnp.log1p(-a_base)
    return {
        "x_prompt": n(ks[0], (BATCH, SEQ, D_MODEL), 1.0),
        "x_sample": n(ks[1], (DEC_BATCH, DEC_SEQ, D_MODEL), 1.0),
        "state_rglru": n(ks[2], (DEPTH, DEC_BATCH, D_RNN), 0.5),
        "state_conv4": n(ks[3], (DEPTH, DEC_BATCH, CONV4_W - 1, D_RNN), 1.0),
        "state_conv3": n(ks[4], (DEPTH, DEC_BATCH, CONV3_W - 1, D_CONV), 1.0),
        "w_in": n(ks[5], (DEPTH, D_MODEL, N_IN), D_MODEL ** -0.5),
        "b_in": n(ks[6], (DEPTH, N_IN), 0.02),
        "conv4_w": n(ks[7], (DEPTH, CONV4_W, D_RNN), CONV4_W ** -0.5),
        "conv4_b": n(ks[8], (DEPTH, D_RNN), 0.02),
        "w_rg_a": n(ks[9], (DEPTH, N_BLOCKS, BLOCK, BLOCK), BLOCK ** -0.5),
        "b_rg_a": n(ks[10], (DEPTH, D_RNN), 0.02),
        "w_rg_x": n(ks[11], (DEPTH, N_BLOCKS, BLOCK, BLOCK), BLOCK ** -0.5),
        "b_rg_x": n(ks[12], (DEPTH, D_RNN), 0.02),
        "rg_lambda": rg_lambda,
        "conv3_w": n(ks[14], (DEPTH, CONV3_W, D_CONV), CONV3_W ** -0.5),
        "w_rnn_out": n(ks[15], (DEPTH, D_RNN, D_MODEL), BETA * D_RNN ** -0.5),
        "w_conv_out": n(ks[16], (DEPTH, D_CONV, D_MODEL), BETA * D_CONV ** -0.5),
        "w_out": n(ks[17], (DEPTH, D_MODEL, D_MODEL), BETA * D_MODEL ** -0.5),
        "ln_g": 1.0 + n(ks[18], (DEPTH, D_MODEL), 0.02),
        "ln_b": n(ks[19], (DEPTH, D_MODEL), 0.02),
    }


def reference(x_prompt, x_sample, state_rglru, state_conv4, state_conv3, w_in, b_in, conv4_w,
              conv4_b, w_rg_a, b_rg_a, w_rg_x, b_rg_x, rg_lambda, conv3_w, w_rnn_out,
              w_conv_out, w_out, ln_g, ln_b):
    bp, tp, _ = x_prompt.shape
    ts = x_sample.shape[1]
    reset_p = jnp.arange(tp) == 0
    reset_s = jnp.zeros((ts,), dtype=bool)
    h0_p = jnp.zeros((bp, D_RNN), x_prompt.dtype)
    c4_p = jnp.zeros((bp, CONV4_W - 1, D_RNN), x_prompt.dtype)
    c3_p = jnp.zeros((bp, CONV3_W - 1, D_CONV), x_prompt.dtype)
    yp, ys = x_prompt, x_sample
    ph, pc4, pc3, sh, sc4, sc3 = [], [], [], [], [], []
    for l in range(DEPTH):
        w = (w_in[l], b_in[l], conv4_w[l], conv4_b[l], w_rg_a[l], b_rg_a[l], w_rg_x[l],
             b_rg_x[l], rg_lambda[l], conv3_w[l], w_rnn_out[l], w_conv_out[l], w_out[l],
             ln_g[l], ln_b[l])
        yp, h, c4, c3 = _layer(yp, h0_p, c4_p, c3_p, reset_p, *w)
        ph.append(h); pc4.append(c4); pc3.append(c3)
        ys, h, c4, c3 = _layer(ys, state_rglru[l], state_conv4[l], state_conv3[l], reset_s, *w)
        sh.append(h); sc4.append(c4); sc3.append(c3)
    return (yp, ys, jnp.stack(ph), jnp.stack(pc4), jnp.stack(pc3),
            jnp.stack(sh), jnp.stack(sc4), jnp.stack(sc3))
```

```python
import functools

import jax
import jax.numpy as jnp
from jax import lax
from jax.experimental import pallas as pl
from jax.experimental.pallas import tpu as pltpu

D_MODEL = 1024
N_BLOCKS = 8
BLOCK = D_MODEL // N_BLOCKS
CONV4_W = 4
CONV3_W = 3
C_RG = 8.0
DEPTH = 4
ALPHA = (2.0 * DEPTH) ** 0.25
LN_EPS = 1e-5

SUBLANES = 8
CHUNK_ROWS = 256
PROMPT_TILE_ROWS = 512
VMEM_LIMIT_BYTES = 56 * 1024 * 1024

V_BIN = 0
V_W4 = 8
V_B4 = 12
V_BA = 13
V_BX = 14
V_LAM = 15
V_W3 = 16
V_LNG = 19
V_LNB = 20
V_ROWS = 24

S_XR, S_GR, S_CB, S_CC, S_CH, S_GC, S_GRNN, S_GCONV = range(8)


def _sigmoid(x):
    return 1.0 / (1.0 + jnp.exp(-x))


def _layer_kernel(x_ref, w_in_ref, w_gate_ref, w_rnn_ref, w_conv_ref, w_out_ref, vec_ref,
                  h0_ref, c4_ref, c3_ref,
                  y_ref, hT_ref, c4n_ref, c3n_ref,
                  h_scr, up4_scr, up3_scr, a_scr, b_scr, hs_scr, negc_scr,
                  *, batch, tile_rows, chunk_rows, reset_first):
    pid = pl.program_id(0)
    n_chunks = tile_rows // chunk_rows
    tail4 = (CONV4_W - 1) * batch
    tail3 = (CONV3_W - 1) * batch
    steps = chunk_rows // batch
    f32 = jnp.float32
    bf16 = jnp.bfloat16

    @pl.when(pid == 0)
    def _init():
        h_scr[...] = h0_ref[...]
        up4_scr[0:tail4, :] = c4_ref[...]
        up3_scr[0:tail3, :] = c3_ref[...]

    nl = -vec_ref[V_LAM:V_LAM + 1, :]
    negc_scr[0:1, :] = -C_RG * (jnp.maximum(nl, 0.0) + jnp.log1p(jnp.exp(-jnp.abs(nl))))

    def vrow(r):
        return vec_ref[r:r + 1, :]

    def chunk_body(c, carry):
        r0 = pl.multiple_of(c * chunk_rows, chunk_rows)
        rows = pl.ds(r0, chunk_rows)
        xf = x_ref[rows, :]
        xb = xf.astype(bf16)

        def proj(k):
            w = w_in_ref[:, k * D_MODEL:(k + 1) * D_MODEL]
            return jnp.dot(xb, w, preferred_element_type=f32) + vrow(V_BIN + k)

        up4_scr[tail4:tail4 + chunk_rows, :] = proj(S_XR)
        xc = up4_scr[0:chunk_rows, :] * vrow(V_W4)
        for k in range(1, CONV4_W):
            xc = xc + up4_scr[k * batch:k * batch + chunk_rows, :] * vrow(V_W4 + k)
        xc = xc + vrow(V_B4)
        up4_scr[0:tail4, :] = up4_scr[chunk_rows:chunk_rows + tail4, :]
        xcb = xc.astype(bf16)

        if reset_first:
            first = jnp.logical_and(pid == 0, c == 0)

        for n in range(N_BLOCKS):
            sl = slice(n * BLOCK, (n + 1) * BLOCK)
            g = jnp.dot(xcb[:, sl], w_gate_ref[n], preferred_element_type=f32)
            r = _sigmoid(g[:, :BLOCK] + vec_ref[V_BA:V_BA + 1, sl])
            i = _sigmoid(g[:, BLOCK:] + vec_ref[V_BX:V_BX + 1, sl])
            a = jnp.exp(r * negc_scr[0:1, sl])
            ix = i * xc[:, sl]
            bb = jnp.sqrt(1.0 - a * a) * ix
            a_scr[:, sl] = a
            if reset_first:
                b_scr[0:batch, sl] = jnp.where(first, ix[0:batch], bb[0:batch])
                b_scr[batch:, sl] = bb[batch:]
            else:
                b_scr[:, sl] = bb

        for gidx in range(batch // SUBLANES):
            g0 = gidx * SUBLANES
            h = h_scr[g0:g0 + SUBLANES, :]
            for t in range(steps):
                row = t * batch + g0
                h = a_scr[row:row + SUBLANES, :] * h + b_scr[row:row + SUBLANES, :]
                hs_scr[row:row + SUBLANES, :] = h
            h_scr[g0:g0 + SUBLANES, :] = h

        gr = proj(S_GR)
        u = (hs_scr[...] * (gr * _sigmoid(gr))).astype(bf16)
        y_rnn = jnp.dot(u, w_rnn_ref[...], preferred_element_type=f32)
        m = _sigmoid(proj(S_GRNN)) * y_rnn

        up3_scr[tail3:tail3 + chunk_rows, :] = proj(S_CC) * proj(S_CH)
        v = up3_scr[0:chunk_rows, :] * vrow(V_W3)
        for k in range(1, CONV3_W):
            v = v + up3_scr[k * batch:k * batch + chunk_rows, :] * vrow(V_W3 + k)
        up3_scr[0:tail3, :] = up3_scr[chunk_rows:chunk_rows + tail3, :]
        gc = proj(S_GC)
        w = (proj(S_CB) * v * (gc * _sigmoid(gc))).astype(bf16)
        y_conv = jnp.dot(w, w_conv_ref[...], preferred_element_type=f32)
        m = m + _sigmoid(proj(S_GCONV)) * y_conv

        out = jnp.dot(m.astype(bf16), w_out_ref[...], preferred_element_type=f32)
        yres = ALPHA * xf + out
        mu = jnp.mean(yres, axis=-1, keepdims=True)
        d = yres - mu
        var = jnp.mean(d * d, axis=-1, keepdims=True)
        y_ref[rows, :] = d * lax.rsqrt(var + LN_EPS) * vrow(V_LNG) + vrow(V_LNB)
        return carry

    lax.fori_loop(0, n_chunks, chunk_body, 0)

    hT_ref[...] = h_scr[...]
    c4n_ref[...] = up4_scr[0:tail4, :]
    c3n_ref[...] = up3_scr[0:tail3, :]


def _run_layer(x, weights, h0, c4, c3, *, batch, tile_rows, reset_first):
    w_in, w_gate, w_rnn, w_conv, w_out, vec = weights
    n_rows = x.shape[0]
    assert n_rows % tile_rows == 0 and tile_rows % CHUNK_ROWS == 0
    assert CHUNK_ROWS % batch == 0 and batch % SUBLANES == 0
    tail4 = (CONV4_W - 1) * batch
    tail3 = (CONV3_W - 1) * batch
    f32 = jnp.float32

    def full(a):
        nd = a.ndim
        return pl.BlockSpec(a.shape, lambda i, _nd=nd: (0,) * _nd, pipeline_mode=pl.Buffered(1))

    kernel = functools.partial(_layer_kernel, batch=batch, tile_rows=tile_rows,
                               chunk_rows=CHUNK_ROWS, reset_first=reset_first)
    row_spec = pl.BlockSpec((tile_rows, D_MODEL), lambda i: (i, 0))
    return pl.pallas_call(
        kernel,
        grid=(n_rows // tile_rows,),
        in_specs=[row_spec, full(w_in), full(w_gate), full(w_rnn), full(w_conv), full(w_out),
                  full(vec), full(h0), full(c4), full(c3)],
        out_specs=[row_spec,
                   pl.BlockSpec((batch, D_MODEL), lambda i: (0, 0)),
                   pl.BlockSpec((tail4, D_MODEL), lambda i: (0, 0)),
                   pl.BlockSpec((tail3, D_MODEL), lambda i: (0, 0))],
        out_shape=[jax.ShapeDtypeStruct((n_rows, D_MODEL), f32),
                   jax.ShapeDtypeStruct((batch, D_MODEL), f32),
                   jax.ShapeDtypeStruct((tail4, D_MODEL), f32),
                   jax.ShapeDtypeStruct((tail3, D_MODEL), f32)],
        scratch_shapes=[
            pltpu.VMEM((batch, D_MODEL), f32),
            pltpu.VMEM((tail4 + CHUNK_ROWS, D_MODEL), f32),
            pltpu.VMEM((tail3 + CHUNK_ROWS, D_MODEL), f32),
            pltpu.VMEM((CHUNK_ROWS, D_MODEL), f32),
            pltpu.VMEM((CHUNK_ROWS, D_MODEL), f32),
            pltpu.VMEM((CHUNK_ROWS, D_MODEL), f32),
            pltpu.VMEM((SUBLANES, D_MODEL), f32),
        ],
        compiler_params=pltpu.CompilerParams(
            dimension_semantics=("arbitrary",),
            vmem_limit_bytes=VMEM_LIMIT_BYTES),
    )(x, w_in, w_gate, w_rnn, w_conv, w_out, vec, h0, c4, c3)


def _time_major(a):
    b, t, d = a.shape
    return jnp.transpose(a, (1, 0, 2)).reshape(t * b, d)


def _batch_major(a, batch):
    n, d = a.shape
    return jnp.transpose(a.reshape(n // batch, batch, d), (1, 0, 2))


def kernel(x_prompt, x_sample, state_rglru, state_conv4, state_conv3, w_in, b_in, conv4_w, conv4_b,
           w_rg_a, b_rg_a, w_rg_x, b_rg_x, rg_lambda, conv3_w, w_rnn_out, w_conv_out, w_out,
           ln_g, ln_b):
    bp = x_prompt.shape[0]
    bs = x_sample.shape[0]
    f32 = jnp.float32
    bf16 = jnp.bfloat16

    yp = _time_major(x_prompt)
    ys = _time_major(x_sample)
    h0_p = jnp.zeros((bp, D_MODEL), f32)
    c4_p = jnp.zeros(((CONV4_W - 1) * bp, D_MODEL), f32)
    c3_p = jnp.zeros(((CONV3_W - 1) * bp, D_MODEL), f32)

    ph, pc4, pc3, sh, sc4, sc3 = [], [], [], [], [], []
    for l in range(DEPTH):
        vec = jnp.concatenate([
            b_in[l].reshape(8, D_MODEL), conv4_w[l], conv4_b[l][None], b_rg_a[l][None],
            b_rg_x[l][None], rg_lambda[l][None], conv3_w[l], ln_g[l][None], ln_b[l][None],
            jnp.zeros((V_ROWS - 21, D_MODEL), f32)], axis=0)
        weights = (w_in[l].astype(bf16),
                   jnp.concatenate([w_rg_a[l], w_rg_x[l]], axis=-1).astype(bf16),
                   w_rnn_out[l].astype(bf16), w_conv_out[l].astype(bf16), w_out[l].astype(bf16),
                   vec)
        yp, h, c4, c3 = _run_layer(yp, weights, h0_p, c4_p, c3_p, batch=bp,
                                   tile_rows=PROMPT_TILE_ROWS, reset_first=True)
        ph.append(h); pc4.append(_batch_major(c4, bp)); pc3.append(_batch_major(c3, bp))
        ys, h, c4, c3 = _run_layer(ys, weights, state_rglru[l], _time_major(state_conv4[l]),
                                   _time_major(state_conv3[l]), batch=bs,
                                   tile_rows=ys.shape[0], reset_first=False)
        sh.append(h); sc4.append(_batch_major(c4, bs)); sc3.append(_batch_major(c3, bs))

    return (_batch_major(yp, bp), _batch_major(ys, bs), jnp.stack(ph), jnp.stack(pc4),
            jnp.stack(pc3), jnp.stack(sh), jnp.stack(sc4), jnp.stack(sc3))
```

```python
import functools

import jax
import jax.numpy as jnp
from jax import lax
from jax.experimental import pallas as pl
from jax.experimental.pallas import tpu as pltpu

D_MODEL = 1024
N_BLOCKS = 8
BLOCK = D_MODEL // N_BLOCKS
CONV4_W = 4
CONV3_W = 3
C_RG = 8.0
DEPTH = 4
ALPHA = (2.0 * DEPTH) ** 0.25
LN_EPS = 1e-5

SUBLANES = 8
CHUNK_ROWS = 512
TILE_ROWS = 512
OUT_SLAB_ROWS = 256
VMEM_LIMIT_BYTES = 56 * 1024 * 1024

V_BIN = 0
V_W4 = 8
V_B4 = 12
V_BA = 13
V_BX = 14
V_LAM = 15
V_W3 = 16
V_LNG = 19
V_LNB = 20
V_ROWS = 24

S_XR, S_GR, S_CB, S_CC, S_CH, S_GC, S_GRNN, S_GCONV = range(8)


def _sigmoid(x):
    return 1.0 / (1.0 + jnp.exp(-x))


def _silu(x):
    return x * _sigmoid(x)


def _layer_kernel(layer_ref, x_ref, w_in_ref, w_gate_ref, w_rnn_ref, w_conv_ref, w_out_ref, vec_ref,
                  h0_ref, c4_ref, c3_ref,
                  y_ref, hT_ref, c4n_ref, c3n_ref,
                  h_scr, up4_scr, up3_scr, a_scr, b_scr, sgr_scr, sgrnn_scr, mconv_scr, lhs_scr,
                  negc_scr,
                  *, batch, tile_rows, chunk_rows, reset_first):
    del layer_ref
    pid = pl.program_id(0)
    n_chunks = tile_rows // chunk_rows
    tail4 = (CONV4_W - 1) * batch
    tail3 = (CONV3_W - 1) * batch
    steps = chunk_rows // batch
    f32 = jnp.float32
    bf16 = jnp.bfloat16

    @pl.when(pid == 0)
    def _init():
        h_scr[...] = h0_ref[...]
        up4_scr[0:tail4, :] = c4_ref[...]
        up3_scr[0:tail3, :] = c3_ref[...]

    nl = -vec_ref[V_LAM:V_LAM + 1, :]
    negc_scr[0:1, :] = -C_RG * (jnp.maximum(nl, 0.0) + jnp.log1p(jnp.exp(-jnp.abs(nl))))

    def vrow(r):
        return vec_ref[r:r + 1, :]

    def chunk(r0, first):
        xb = x_ref[pl.ds(r0, chunk_rows), :].astype(bf16)

        def proj(k):
            w = w_in_ref[:, k * D_MODEL:(k + 1) * D_MODEL]
            return jnp.dot(xb, w, preferred_element_type=f32) + vrow(V_BIN + k)

        up4_scr[tail4:tail4 + chunk_rows, :] = proj(S_XR)
        xc = up4_scr[0:chunk_rows, :] * vrow(V_W4)
        for k in range(1, CONV4_W):
            xc = xc + up4_scr[k * batch:k * batch + chunk_rows, :] * vrow(V_W4 + k)
        b_scr[...] = xc + vrow(V_B4)
        up4_scr[0:tail4, :] = up4_scr[chunk_rows:chunk_rows + tail4, :]

        def gate_block(n):
            sl = slice(n * BLOCK, (n + 1) * BLOCK)
            xcn = b_scr[:, sl]
            g = jnp.dot(xcn.astype(bf16), w_gate_ref[n], preferred_element_type=f32)
            r = _sigmoid(g[:, :BLOCK] + vec_ref[V_BA:V_BA + 1, sl])
            i = _sigmoid(g[:, BLOCK:] + vec_ref[V_BX:V_BX + 1, sl])
            a = jnp.exp(r * negc_scr[0:1, sl])
            ix = i * xcn
            s = 1.0 - a * a
            bb = jnp.where(s > 0.0, s * lax.rsqrt(s), 0.0) * ix
            a_scr[:, sl] = a
            if first is not None:
                b_scr[0:batch, sl] = jnp.where(first, ix[0:batch], bb[0:batch])
                b_scr[batch:, sl] = bb[batch:]
            else:
                b_scr[:, sl] = bb

        def conv_cc():
            up3_scr[tail3:tail3 + chunk_rows, :] = proj(S_CC)

        def conv_ch():
            up3_scr[tail3:tail3 + chunk_rows, :] = up3_scr[tail3:tail3 + chunk_rows, :] * proj(S_CH)

        def conv_gc():
            mconv_scr[...] = _silu(proj(S_GC))

        def conv_cb():
            v = up3_scr[0:chunk_rows, :] * vrow(V_W3)
            for k in range(1, CONV3_W):
                v = v + up3_scr[k * batch:k * batch + chunk_rows, :] * vrow(V_W3 + k)
            lhs_scr[...] = (proj(S_CB) * v * mconv_scr[...]).astype(bf16)
            up3_scr[0:tail3, :] = up3_scr[chunk_rows:chunk_rows + tail3, :]

        def conv_out():
            mconv_scr[...] = jnp.dot(lhs_scr[...], w_conv_ref[...], preferred_element_type=f32)

        def conv_merge():
            mconv_scr[...] = _sigmoid(proj(S_GCONV)) * mconv_scr[...]

        def rnn_gr():
            sgr_scr[...] = _silu(proj(S_GR))

        def rnn_grnn():
            sgrnn_scr[...] = _sigmoid(proj(S_GRNN))

        between = [conv_cc, conv_ch, conv_gc, conv_cb, conv_out, conv_merge, rnn_gr, rnn_grnn]
        for n in range(N_BLOCKS):
            gate_block(n)
            between[n]()

        for gidx in range(batch // SUBLANES):
            g0 = gidx * SUBLANES
            h = h_scr[g0:g0 + SUBLANES, :]
            for t in range(steps):
                row = t * batch + g0
                h = a_scr[row:row + SUBLANES, :] * h + b_scr[row:row + SUBLANES, :]
                b_scr[row:row + SUBLANES, :] = h
            h_scr[g0:g0 + SUBLANES, :] = h
        lhs_scr[...] = (b_scr[...] * sgr_scr[...]).astype(bf16)

        slabs = [pl.ds(s0, OUT_SLAB_ROWS) for s0 in range(0, chunk_rows, OUT_SLAB_ROWS)]
        y_rnn = [jnp.dot(lhs_scr[sl, :], w_rnn_ref[...], preferred_element_type=f32)
                 for sl in slabs]
        for s0, sl, yr in zip(range(0, chunk_rows, OUT_SLAB_ROWS), slabs, y_rnn):
            m = sgrnn_scr[sl, :] * yr + mconv_scr[sl, :]
            out = jnp.dot(m.astype(bf16), w_out_ref[...], preferred_element_type=f32)
            yres = ALPHA * x_ref[pl.ds(r0 + s0, OUT_SLAB_ROWS), :] + out
            mu = jnp.mean(yres, axis=-1, keepdims=True)
            d = yres - mu
            var = jnp.mean(d * d, axis=-1, keepdims=True)
            y_ref[pl.ds(r0 + s0, OUT_SLAB_ROWS), :] = (
                d * lax.rsqrt(var + LN_EPS) * vrow(V_LNG) + vrow(V_LNB))

    if n_chunks == 1:
        chunk(0, (pid == 0) if reset_first else None)
    else:
        def chunk_body(c, carry):
            first = jnp.logical_and(pid == 0, c == 0) if reset_first else None
            chunk(pl.multiple_of(c * chunk_rows, chunk_rows), first)
            return carry
        lax.fori_loop(0, n_chunks, chunk_body, 0)

    hT_ref[...] = h_scr[...]
    c4n_ref[...] = up4_scr[0:tail4, :]
    c3n_ref[...] = up3_scr[0:tail3, :]


def _run_layer(layer, x, weights, h0, c4, c3, *, batch, reset_first):
    n_rows = x.shape[0]
    assert n_rows % TILE_ROWS == 0 and TILE_ROWS % CHUNK_ROWS == 0
    assert CHUNK_ROWS % batch == 0 and batch % SUBLANES == 0
    assert CHUNK_ROWS % OUT_SLAB_ROWS == 0
    tail4 = (CONV4_W - 1) * batch
    tail3 = (CONV3_W - 1) * batch
    f32 = jnp.float32

    def layer_block(a):
        nd = a.ndim
        return pl.BlockSpec((None,) + a.shape[1:],
                            lambda i, l_ref, _nd=nd: (l_ref[0],) + (0,) * (_nd - 1),
                            pipeline_mode=pl.Buffered(1))

    def const_block(rows):
        return pl.BlockSpec((rows, D_MODEL), lambda i, l_ref: (0, 0))

    kernel = functools.partial(_layer_kernel, batch=batch, tile_rows=TILE_ROWS,
                               chunk_rows=CHUNK_ROWS, reset_first=reset_first)
    row_spec = pl.BlockSpec((TILE_ROWS, D_MODEL), lambda i, l_ref: (i, 0))
    chunk_f32 = pltpu.VMEM((CHUNK_ROWS, D_MODEL), f32)
    grid_spec = pltpu.PrefetchScalarGridSpec(
        num_scalar_prefetch=1,
        grid=(n_rows // TILE_ROWS,),
        in_specs=[row_spec] + [layer_block(a) for a in weights]
                 + [layer_block(h0), layer_block(c4), layer_block(c3)],
        out_specs=[row_spec, const_block(batch), const_block(tail4), const_block(tail3)],
        scratch_shapes=[
            pltpu.VMEM((batch, D_MODEL), f32),
            pltpu.VMEM((tail4 + CHUNK_ROWS, D_MODEL), f32),
            pltpu.VMEM((tail3 + CHUNK_ROWS, D_MODEL), f32),
            chunk_f32,
            chunk_f32,
            chunk_f32,
            chunk_f32,
            chunk_f32,
            pltpu.VMEM((CHUNK_ROWS, D_MODEL), jnp.bfloat16),
            pltpu.VMEM((SUBLANES, D_MODEL), f32),
        ])
    return pl.pallas_call(
        kernel,
        grid_spec=grid_spec,
        out_shape=[jax.ShapeDtypeStruct((n_rows, D_MODEL), f32),
                   jax.ShapeDtypeStruct((batch, D_MODEL), f32),
                   jax.ShapeDtypeStruct((tail4, D_MODEL), f32),
                   jax.ShapeDtypeStruct((tail3, D_MODEL), f32)],
        compiler_params=pltpu.CompilerParams(
            dimension_semantics=("arbitrary",),
            vmem_limit_bytes=VMEM_LIMIT_BYTES),
    )(jnp.full((1,), layer, jnp.int32), x, *weights, h0, c4, c3)


def _time_major(a):
    b, t, d = a.shape[-3:]
    return jnp.swapaxes(a, -3, -2).reshape(a.shape[:-3] + (t * b, d))


def _batch_major(a, batch):
    n, d = a.shape[-2:]
    return jnp.swapaxes(a.reshape(a.shape[:-2] + (n // batch, batch, d)), -3, -2)


def kernel(x_prompt, x_sample, state_rglru, state_conv4, state_conv3, w_in, b_in, conv4_w, conv4_b,
           w_rg_a, b_rg_a, w_rg_x, b_rg_x, rg_lambda, conv3_w, w_rnn_out, w_conv_out, w_out,
           ln_g, ln_b):
    bp = x_prompt.shape[0]
    bs = x_sample.shape[0]
    f32 = jnp.float32
    bf16 = jnp.bfloat16

    vec = jnp.concatenate([
        b_in.reshape(DEPTH, 8, D_MODEL), conv4_w, conv4_b[:, None], b_rg_a[:, None],
        b_rg_x[:, None], rg_lambda[:, None], conv3_w, ln_g[:, None], ln_b[:, None],
        jnp.zeros((DEPTH, V_ROWS - 21, D_MODEL), f32)], axis=1)
    weights = (w_in.astype(bf16),
               jnp.concatenate([w_rg_a, w_rg_x], axis=-1).astype(bf16),
               w_rnn_out.astype(bf16), w_conv_out.astype(bf16), w_out.astype(bf16), vec)

    yp = _time_major(x_prompt)
    ys = _time_major(x_sample)
    h0_p = jnp.zeros((DEPTH, bp, D_MODEL), f32)
    c4_p = jnp.zeros((DEPTH, (CONV4_W - 1) * bp, D_MODEL), f32)
    c3_p = jnp.zeros((DEPTH, (CONV3_W - 1) * bp, D_MODEL), f32)
    c4_s = _time_major(state_conv4)
    c3_s = _time_major(state_conv3)

    ph, pc4, pc3, sh, sc4, sc3 = [], [], [], [], [], []
    for l in range(DEPTH):
        yp, h, c4, c3 = _run_layer(l, yp, weights, h0_p, c4_p, c3_p, batch=bp, reset_first=True)
        ph.append(h); pc4.append(c4); pc3.append(c3)
        ys, h, c4, c3 = _run_layer(l, ys, weights, state_rglru, c4_s, c3_s, batch=bs,
                                   reset_first=False)
        sh.append(h); sc4.append(c4); sc3.append(c3)

    return (_batch_major(yp, bp), _batch_major(ys, bs),
            jnp.stack(ph), _batch_major(jnp.stack(pc4), bp), _batch_major(jnp.stack(pc3), bp),
            jnp.stack(sh), _batch_major(jnp.stack(sc4), bs), _batch_major(jnp.stack(sc3), bs))
```

```python
import functools

import jax
import jax.numpy as jnp
from jax import lax
from jax.experimental import pallas as pl
from jax.experimental.pallas import tpu as pltpu

D_MODEL = 1024
N_BLOCKS = 8
BLOCK = D_MODEL // N_BLOCKS
CONV4_W = 4
CONV3_W = 3
C_RG = 8.0
DEPTH = 4
ALPHA = (2.0 * DEPTH) ** 0.25
LN_EPS = 1e-5

SUBLANES = 8
LANES = 128
CHUNK_ROWS = 512
TILE_ROWS = 512
OUT_SLAB_ROWS = 256
VMEM_LIMIT_BYTES = 56 * 1024 * 1024

V_BIN = 0
V_W4 = 8
V_B4 = 12
V_BA = 13
V_BX = 14
V_LAM = 15
V_W3 = 16
V_LNG = 19
V_LNB = 20
V_ROWS = 24

S_XR, S_GR, S_CB, S_CC, S_CH, S_GC, S_GRNN, S_GCONV = range(8)


def _sigmoid(x):
    return 1.0 / (1.0 + jnp.exp(-x))


def _silu(x):
    return x * _sigmoid(x)


def _layer_kernel(layer_ref, x_ref, w_in_ref, w_gate_ref, w_rnn_ref, w_conv_ref, w_out_ref, vec_ref,
                  h0_ref, c4_ref, c3_ref,
                  y_ref, hT_ref, c4n_ref, c3n_ref,
                  h_scr, up4_scr, up3_scr, a_scr, b_scr, sgr_scr, sgrnn_scr, mconv_scr, lhs_scr,
                  negc_scr, *relayout_scr,
                  batch, tile_rows, chunk_rows, reset_first, x_batch_major, y_batch_major):
    del layer_ref
    pid = pl.program_id(0)
    n_chunks = tile_rows // chunk_rows
    tile_steps = tile_rows // batch
    relayout_scr = list(relayout_scr)
    xtm_scr = relayout_scr.pop(0) if x_batch_major else None
    ytm_scr = relayout_scr.pop(0) if y_batch_major else None

    def lane_slab(j):
        return slice(j * LANES, (j + 1) * LANES)

    if x_batch_major:
        for bi in range(batch):
            for j in range(D_MODEL // LANES):
                xtm_scr[j, pl.ds(bi, tile_steps, stride=batch), :] = x_ref[bi, :, lane_slab(j)]

    def load_x(row0, n):
        if x_batch_major:
            return jnp.concatenate(
                [xtm_scr[j, pl.ds(row0, n), :] for j in range(D_MODEL // LANES)], axis=-1)
        return x_ref[pl.ds(row0, n), :]

    def store_y(row0, n, val):
        if y_batch_major:
            for j in range(D_MODEL // LANES):
                ytm_scr[j, pl.ds(row0, n), :] = val[:, lane_slab(j)]
        else:
            y_ref[pl.ds(row0, n), :] = val
    tail4 = (CONV4_W - 1) * batch
    tail3 = (CONV3_W - 1) * batch
    steps = chunk_rows // batch
    f32 = jnp.float32
    bf16 = jnp.bfloat16

    @pl.when(pid == 0)
    def _init():
        h_scr[...] = h0_ref[...]
        up4_scr[0:tail4, :] = c4_ref[...]
        up3_scr[0:tail3, :] = c3_ref[...]

    nl = -vec_ref[V_LAM:V_LAM + 1, :]
    negc_scr[0:1, :] = -C_RG * (jnp.maximum(nl, 0.0) + jnp.log1p(jnp.exp(-jnp.abs(nl))))

    def vrow(r):
        return vec_ref[r:r + 1, :]

    def chunk(r0, first):
        xb = load_x(r0, chunk_rows).astype(bf16)

        def proj(k):
            w = w_in_ref[:, k * D_MODEL:(k + 1) * D_MODEL]
            return jnp.dot(xb, w, preferred_element_type=f32) + vrow(V_BIN + k)

        up4_scr[tail4:tail4 + chunk_rows, :] = proj(S_XR)
        xc = up4_scr[0:chunk_rows, :] * vrow(V_W4)
        for k in range(1, CONV4_W):
            xc = xc + up4_scr[k * batch:k * batch + chunk_rows, :] * vrow(V_W4 + k)
        b_scr[...] = xc + vrow(V_B4)
        up4_scr[0:tail4, :] = up4_scr[chunk_rows:chunk_rows + tail4, :]

        def gate_block(n):
            sl = slice(n * BLOCK, (n + 1) * BLOCK)
            xcn = b_scr[:, sl]
            g = jnp.dot(xcn.astype(bf16), w_gate_ref[n], preferred_element_type=f32)
            r = _sigmoid(g[:, :BLOCK] + vec_ref[V_BA:V_BA + 1, sl])
            i = _sigmoid(g[:, BLOCK:] + vec_ref[V_BX:V_BX + 1, sl])
            a = jnp.exp(r * negc_scr[0:1, sl])
            ix = i * xcn
            s = 1.0 - a * a
            bb = jnp.where(s > 0.0, s * lax.rsqrt(s), 0.0) * ix
            a_scr[:, sl] = a
            if first is not None:
                b_scr[0:batch, sl] = jnp.where(first, ix[0:batch], bb[0:batch])
                b_scr[batch:, sl] = bb[batch:]
            else:
                b_scr[:, sl] = bb

        def conv_cc():
            up3_scr[tail3:tail3 + chunk_rows, :] = proj(S_CC)

        def conv_ch():
            up3_scr[tail3:tail3 + chunk_rows, :] = up3_scr[tail3:tail3 + chunk_rows, :] * proj(S_CH)

        def conv_gc():
            mconv_scr[...] = _silu(proj(S_GC))

        def conv_cb():
            v = up3_scr[0:chunk_rows, :] * vrow(V_W3)
            for k in range(1, CONV3_W):
                v = v + up3_scr[k * batch:k * batch + chunk_rows, :] * vrow(V_W3 + k)
            lhs_scr[...] = (proj(S_CB) * v * mconv_scr[...]).astype(bf16)
            up3_scr[0:tail3, :] = up3_scr[chunk_rows:chunk_rows + tail3, :]

        def conv_out():
            mconv_scr[...] = jnp.dot(lhs_scr[...], w_conv_ref[...], preferred_element_type=f32)

        def conv_merge():
            mconv_scr[...] = _sigmoid(proj(S_GCONV)) * mconv_scr[...]

        def rnn_gr():
            sgr_scr[...] = _silu(proj(S_GR))

        def rnn_grnn():
            sgrnn_scr[...] = _sigmoid(proj(S_GRNN))

        between = [conv_cc, conv_ch, conv_gc, conv_cb, conv_out, conv_merge, rnn_gr, rnn_grnn]
        for n in range(N_BLOCKS):
            gate_block(n)
            between[n]()

        for gidx in range(batch // SUBLANES):
            g0 = gidx * SUBLANES
            h = h_scr[g0:g0 + SUBLANES, :]
            for t in range(steps):
                row = t * batch + g0
                h = a_scr[row:row + SUBLANES, :] * h + b_scr[row:row + SUBLANES, :]
                b_scr[row:row + SUBLANES, :] = h
            h_scr[g0:g0 + SUBLANES, :] = h
        lhs_scr[...] = (b_scr[...] * sgr_scr[...]).astype(bf16)

        slabs = [pl.ds(s0, OUT_SLAB_ROWS) for s0 in range(0, chunk_rows, OUT_SLAB_ROWS)]
        y_rnn = [jnp.dot(lhs_scr[sl, :], w_rnn_ref[...], preferred_element_type=f32)
                 for sl in slabs]
        for s0, sl, yr in zip(range(0, chunk_rows, OUT_SLAB_ROWS), slabs, y_rnn):
            m = sgrnn_scr[sl, :] * yr + mconv_scr[sl, :]
            out = jnp.dot(m.astype(bf16), w_out_ref[...], preferred_element_type=f32)
            yres = ALPHA * load_x(r0 + s0, OUT_SLAB_ROWS) + out
            mu = jnp.mean(yres, axis=-1, keepdims=True)
            d = yres - mu
            var = jnp.mean(d * d, axis=-1, keepdims=True)
            store_y(r0 + s0, OUT_SLAB_ROWS,
                    d * lax.rsqrt(var + LN_EPS) * vrow(V_LNG) + vrow(V_LNB))

    if n_chunks == 1:
        chunk(0, (pid == 0) if reset_first else None)
    else:
        def chunk_body(c, carry):
            first = jnp.logical_and(pid == 0, c == 0) if reset_first else None
            chunk(pl.multiple_of(c * chunk_rows, chunk_rows), first)
            return carry
        lax.fori_loop(0, n_chunks, chunk_body, 0)

    if y_batch_major:
        for bi in range(batch):
            for j in range(D_MODEL // LANES):
                y_ref[bi, :, lane_slab(j)] = ytm_scr[j, pl.ds(bi, tile_steps, stride=batch), :]

    hT_ref[...] = h_scr[...]
    c4n_ref[...] = up4_scr[0:tail4, :]
    c3n_ref[...] = up3_scr[0:tail3, :]


def _run_layer(layer, x, weights, h0, c4, c3, *, batch, reset_first,
               x_batch_major=False, y_batch_major=False):
    n_rows = x.shape[0] * x.shape[1] if x_batch_major else x.shape[0]
    assert n_rows % TILE_ROWS == 0 and TILE_ROWS % CHUNK_ROWS == 0
    assert CHUNK_ROWS % batch == 0 and batch % SUBLANES == 0
    assert CHUNK_ROWS % OUT_SLAB_ROWS == 0
    tile_steps = TILE_ROWS // batch
    tail4 = (CONV4_W - 1) * batch
    tail3 = (CONV3_W - 1) * batch
    f32 = jnp.float32

    def layer_block(a):
        nd = a.ndim
        return pl.BlockSpec((None,) + a.shape[1:],
                            lambda i, l_ref, _nd=nd: (l_ref[0],) + (0,) * (_nd - 1),
                            pipeline_mode=pl.Buffered(1))

    def const_block(rows):
        return pl.BlockSpec((rows, D_MODEL), lambda i, l_ref: (0, 0))

    kernel = functools.partial(_layer_kernel, batch=batch, tile_rows=TILE_ROWS,
                               chunk_rows=CHUNK_ROWS, reset_first=reset_first,
                               x_batch_major=x_batch_major, y_batch_major=y_batch_major)
    row_spec = pl.BlockSpec((TILE_ROWS, D_MODEL), lambda i, l_ref: (i, 0))
    bm_spec = pl.BlockSpec((batch, tile_steps, D_MODEL), lambda i, l_ref: (0, i, 0))
    bm_shape = jax.ShapeDtypeStruct((batch, n_rows // batch, D_MODEL), f32)
    relayout = pltpu.VMEM((D_MODEL // LANES, TILE_ROWS, LANES), f32)
    chunk_f32 = pltpu.VMEM((CHUNK_ROWS, D_MODEL), f32)
    grid_spec = pltpu.PrefetchScalarGridSpec(
        num_scalar_prefetch=1,
        grid=(n_rows // TILE_ROWS,),
        in_specs=[bm_spec if x_batch_major else row_spec] + [layer_block(a) for a in weights]
                 + [layer_block(h0), layer_block(c4), layer_block(c3)],
        out_specs=[bm_spec if y_batch_major else row_spec,
                   const_block(batch), const_block(tail4), const_block(tail3)],
        scratch_shapes=[
            pltpu.VMEM((batch, D_MODEL), f32),
            pltpu.VMEM((tail4 + CHUNK_ROWS, D_MODEL), f32),
            pltpu.VMEM((tail3 + CHUNK_ROWS, D_MODEL), f32),
            chunk_f32,
            chunk_f32,
            chunk_f32,
            chunk_f32,
            chunk_f32,
            pltpu.VMEM((CHUNK_ROWS, D_MODEL), jnp.bfloat16),
            pltpu.VMEM((SUBLANES, D_MODEL), f32),
        ] + [relayout] * (int(x_batch_major) + int(y_batch_major)))
    return pl.pallas_call(
        kernel,
        grid_spec=grid_spec,
        out_shape=[bm_shape if y_batch_major else jax.ShapeDtypeStruct((n_rows, D_MODEL), f32),
                   jax.ShapeDtypeStruct((batch, D_MODEL), f32),
                   jax.ShapeDtypeStruct((tail4, D_MODEL), f32),
                   jax.ShapeDtypeStruct((tail3, D_MODEL), f32)],
        compiler_params=pltpu.CompilerParams(
            dimension_semantics=("arbitrary",),
            vmem_limit_bytes=VMEM_LIMIT_BYTES),
    )(jnp.full((1,), layer, jnp.int32), x, *weights, h0, c4, c3)


def _time_major(a):
    b, t, d = a.shape[-3:]
    return jnp.swapaxes(a, -3, -2).reshape(a.shape[:-3] + (t * b, d))


def _batch_major(a, batch):
    n, d = a.shape[-2:]
    return jnp.swapaxes(a.reshape(a.shape[:-2] + (n // batch, batch, d)), -3, -2)


def kernel(x_prompt, x_sample, state_rglru, state_conv4, state_conv3, w_in, b_in, conv4_w, conv4_b,
           w_rg_a, b_rg_a, w_rg_x, b_rg_x, rg_lambda, conv3_w, w_rnn_out, w_conv_out, w_out,
           ln_g, ln_b):
    bp = x_prompt.shape[0]
    bs = x_sample.shape[0]
    f32 = jnp.float32
    bf16 = jnp.bfloat16

    vec = jnp.concatenate([
        b_in.reshape(DEPTH, 8, D_MODEL), conv4_w, conv4_b[:, None], b_rg_a[:, None],
        b_rg_x[:, None], rg_lambda[:, None], conv3_w, ln_g[:, None], ln_b[:, None],
        jnp.zeros((DEPTH, V_ROWS - 21, D_MODEL), f32)], axis=1)
    weights = (w_in.astype(bf16),
               jnp.concatenate([w_rg_a, w_rg_x], axis=-1).astype(bf16),
               w_rnn_out.astype(bf16), w_conv_out.astype(bf16), w_out.astype(bf16), vec)

    yp = x_prompt
    ys = _time_major(x_sample)
    h0_p = jnp.zeros((DEPTH, bp, D_MODEL), f32)
    c4_p = jnp.zeros((DEPTH, (CONV4_W - 1) * bp, D_MODEL), f32)
    c3_p = jnp.zeros((DEPTH, (CONV3_W - 1) * bp, D_MODEL), f32)
    c4_s = _time_major(state_conv4)
    c3_s = _time_major(state_conv3)

    ph, pc4, pc3, sh, sc4, sc3 = [], [], [], [], [], []
    for l in range(DEPTH):
        yp, h, c4, c3 = _run_layer(l, yp, weights, h0_p, c4_p, c3_p, batch=bp, reset_first=True,
                                   x_batch_major=(l == 0), y_batch_major=(l == DEPTH - 1))
        ph.append(h); pc4.append(c4); pc3.append(c3)
        ys, h, c4, c3 = _run_layer(l, ys, weights, state_rglru, c4_s, c3_s, batch=bs,
                                   reset_first=False)
        sh.append(h); sc4.append(c4); sc3.append(c3)

    return (yp, _batch_major(ys, bs),
            jnp.stack(ph), _batch_major(jnp.stack(pc4), bp), _batch_major(jnp.stack(pc3), bp),
            jnp.stack(sh), _batch_major(jnp.stack(sc4), bs), _batch_major(jnp.stack(sc3), bs))
```

```python
import functools

import jax
import jax.numpy as jnp
from jax import lax
from jax.experimental import pallas as pl
from jax.experimental.pallas import tpu as pltpu

D_MODEL = 1024
N_BLOCKS = 8
BLOCK = D_MODEL // N_BLOCKS
CONV4_W = 4
CONV3_W = 3
C_RG = 8.0
DEPTH = 4
ALPHA = (2.0 * DEPTH) ** 0.25
LN_EPS = 1e-5

SUBLANES = 8
LANES = 128
N_SLABS = D_MODEL // LANES
TILE_ROWS = 512
OUT_SLAB_ROWS = 256
VMEM_LIMIT_BYTES = 56 * 1024 * 1024

V_BIN = 0
V_W4 = 8
V_B4 = 12
V_BA = 13
V_BX = 14
V_LAM = 15
V_W3 = 16
V_LNG = 19
V_LNB = 20
V_ROWS = 24

S_XR, S_GR, S_CB, S_CC, S_CH, S_GC, S_GRNN, S_GCONV = range(8)


def _sigmoid(x):
    return 1.0 / (1.0 + jnp.exp(-x))


def _silu(x):
    return x * _sigmoid(x)


def _lane_slab(j):
    return slice(j * LANES, (j + 1) * LANES)


def _layer_kernel(layer_ref, *refs, batch, tile_rows, reset_first, x_batch_major, y_batch_major,
                  pipelined):
    del layer_ref
    refs = list(refs)
    x_ref = refs.pop(0)
    x_first_ref = refs.pop(0) if pipelined else None
    (w_in_ref, w_gate_ref, w_rnn_ref, w_conv_ref, w_out_ref, vec_ref, h0_ref, c4_ref, c3_ref,
     y_ref, hT_ref, c4n_ref, c3n_ref,
     h_scr, up4_scr, up3_scr, a_scr, b_scr, sgr_scr, sgrnn_scr, mconv_scr, lhs_scr, xb_scr,
     negc_scr) = refs[:24]
    extra = refs[24:]
    x_copied = x_batch_major or pipelined
    xtm_scr = extra.pop(0) if x_copied else None
    ytm_scr = extra.pop(0) if y_batch_major else None

    pid = pl.program_id(0)
    tile_steps = tile_rows // batch
    tail4 = (CONV4_W - 1) * batch
    tail3 = (CONV3_W - 1) * batch
    f32 = jnp.float32
    bf16 = jnp.bfloat16
    slab_starts = range(0, tile_rows, OUT_SLAB_ROWS)

    def vrow(r):
        return vec_ref[r:r + 1, :]

    def load_x(row0, n):
        if x_copied:
            return jnp.concatenate([xtm_scr[j, pl.ds(row0, n), :] for j in range(N_SLABS)], axis=-1)
        return x_ref[pl.ds(row0, n), :]

    def store_y(row0, n, val):
        if y_batch_major:
            for j in range(N_SLABS):
                ytm_scr[j, pl.ds(row0, n), :] = val[:, _lane_slab(j)]
        else:
            y_ref[pl.ds(row0, n), :] = val

    def front(src_ref):
        if x_batch_major:
            for bi in range(batch):
                for j in range(N_SLABS):
                    xtm_scr[j, pl.ds(bi, tile_steps, stride=batch), :] = src_ref[bi, :, _lane_slab(j)]
        elif x_copied:
            for j in range(N_SLABS):
                xtm_scr[j, :, :] = src_ref[:, _lane_slab(j)]
        xb_scr[...] = load_x(0, tile_rows).astype(bf16)
        up4_scr[tail4:tail4 + tile_rows, :] = proj(S_XR)

    def proj(k):
        w = w_in_ref[:, k * D_MODEL:(k + 1) * D_MODEL]
        return jnp.dot(xb_scr[...], w, preferred_element_type=f32) + vrow(V_BIN + k)

    def gate_block(n, first):
        sl = slice(n * BLOCK, (n + 1) * BLOCK)
        xcn = b_scr[:, sl]
        g = jnp.dot(xcn.astype(bf16), w_gate_ref[n], preferred_element_type=f32)
        r = _sigmoid(g[:, :BLOCK] + vec_ref[V_BA:V_BA + 1, sl])
        i = _sigmoid(g[:, BLOCK:] + vec_ref[V_BX:V_BX + 1, sl])
        a = jnp.exp(r * negc_scr[0:1, sl])
        ix = i * xcn
        s = 1.0 - a * a
        bb = jnp.where(s > 0.0, s * lax.rsqrt(s), 0.0) * ix
        a_scr[:, sl] = a
        if first is not None:
            b_scr[0:batch, sl] = jnp.where(first, ix[0:batch], bb[0:batch])
            b_scr[batch:, sl] = bb[batch:]
        else:
            b_scr[:, sl] = bb

    def conv_cc():
        up3_scr[tail3:tail3 + tile_rows, :] = proj(S_CC)

    def conv_ch():
        up3_scr[tail3:tail3 + tile_rows, :] = up3_scr[tail3:tail3 + tile_rows, :] * proj(S_CH)

    def conv_gc():
        mconv_scr[...] = _silu(proj(S_GC))

    def conv_cb():
        v = up3_scr[0:tile_rows, :] * vrow(V_W3)
        for k in range(1, CONV3_W):
            v = v + up3_scr[k * batch:k * batch + tile_rows, :] * vrow(V_W3 + k)
        lhs_scr[...] = (proj(S_CB) * v * mconv_scr[...]).astype(bf16)
        up3_scr[0:tail3, :] = up3_scr[tile_rows:tile_rows + tail3, :]

    def conv_out():
        mconv_scr[...] = jnp.dot(lhs_scr[...], w_conv_ref[...], preferred_element_type=f32)

    def conv_merge():
        mconv_scr[...] = _sigmoid(proj(S_GCONV)) * mconv_scr[...]

    def rnn_gr():
        sgr_scr[...] = _silu(proj(S_GR))

    def rnn_grnn():
        sgrnn_scr[...] = _sigmoid(proj(S_GRNN))

    def back(first):
        xc = up4_scr[0:tile_rows, :] * vrow(V_W4)
        for k in range(1, CONV4_W):
            xc = xc + up4_scr[k * batch:k * batch + tile_rows, :] * vrow(V_W4 + k)
        b_scr[...] = xc + vrow(V_B4)
        up4_scr[0:tail4, :] = up4_scr[tile_rows:tile_rows + tail4, :]

        between = [conv_cc, conv_ch, conv_gc, conv_cb, conv_out, conv_merge, rnn_gr, rnn_grnn]
        for n in range(N_BLOCKS):
            gate_block(n, first)
            between[n]()

        for gidx in range(batch // SUBLANES):
            g0 = gidx * SUBLANES
            h = h_scr[g0:g0 + SUBLANES, :]
            for t in range(tile_steps):
                row = t * batch + g0
                h = a_scr[row:row + SUBLANES, :] * h + b_scr[row:row + SUBLANES, :]
                b_scr[row:row + SUBLANES, :] = h
            h_scr[g0:g0 + SUBLANES, :] = h
        lhs_scr[...] = (b_scr[...] * sgr_scr[...]).astype(bf16)

        slabs = [pl.ds(s0, OUT_SLAB_ROWS) for s0 in slab_starts]
        y_rnn = [jnp.dot(lhs_scr[sl, :], w_rnn_ref[...], preferred_element_type=f32)
                 for sl in slabs]
        for s0, sl, yr in zip(slab_starts, slabs, y_rnn):
            m = sgrnn_scr[sl, :] * yr + mconv_scr[sl, :]
            out = jnp.dot(m.astype(bf16), w_out_ref[...], preferred_element_type=f32)
            sgr_scr[sl, :] = ALPHA * load_x(s0, OUT_SLAB_ROWS) + out

    def layer_norm():
        for s0 in slab_starts:
            yres = sgr_scr[pl.ds(s0, OUT_SLAB_ROWS), :]
            mu = jnp.mean(yres, axis=-1, keepdims=True)
            d = yres - mu
            var = jnp.mean(d * d, axis=-1, keepdims=True)
            store_y(s0, OUT_SLAB_ROWS, d * lax.rsqrt(var + LN_EPS) * vrow(V_LNG) + vrow(V_LNB))
        if y_batch_major:
            for bi in range(batch):
                for j in range(N_SLABS):
                    y_ref[bi, :, _lane_slab(j)] = ytm_scr[j, pl.ds(bi, tile_steps, stride=batch), :]

    @pl.when(pid == 0)
    def _init():
        h_scr[...] = h0_ref[...]
        up4_scr[0:tail4, :] = c4_ref[...]
        up3_scr[0:tail3, :] = c3_ref[...]
        if pipelined:
            front(x_first_ref)

    nl = -vec_ref[V_LAM:V_LAM + 1, :]
    negc_scr[0:1, :] = -C_RG * (jnp.maximum(nl, 0.0) + jnp.log1p(jnp.exp(-jnp.abs(nl))))

    first = (pid == 0) if reset_first else None
    if pipelined:
        back(first)
        front(x_ref)
        layer_norm()
    else:
        front(x_ref)
        back(first)
        layer_norm()

    hT_ref[...] = h_scr[...]
    c4n_ref[...] = up4_scr[0:tail4, :]
    c3n_ref[...] = up3_scr[0:tail3, :]


def _run_layer(layer, x, weights, h0, c4, c3, *, batch, reset_first,
               x_batch_major=False, y_batch_major=False, pipelined=False):
    n_rows = x.shape[0] * x.shape[1] if x_batch_major else x.shape[0]
    assert n_rows % TILE_ROWS == 0 and TILE_ROWS % batch == 0 and batch % SUBLANES == 0
    assert TILE_ROWS % OUT_SLAB_ROWS == 0
    n_steps = n_rows // TILE_ROWS
    tile_steps = TILE_ROWS // batch
    tail4 = (CONV4_W - 1) * batch
    tail3 = (CONV3_W - 1) * batch
    f32 = jnp.float32

    def layer_block(a):
        nd = a.ndim
        return pl.BlockSpec((None,) + a.shape[1:],
                            lambda i, l_ref, _nd=nd: (l_ref[0],) + (0,) * (_nd - 1),
                            pipeline_mode=pl.Buffered(1))

    def const_block(rows):
        return pl.BlockSpec((rows, D_MODEL), lambda i, l_ref: (0, 0))

    def act_spec(batch_major, tile_index, **kw):
        if batch_major:
            return pl.BlockSpec((batch, tile_steps, D_MODEL),
                                lambda i, l_ref: (0, tile_index(i), 0), **kw)
        return pl.BlockSpec((TILE_ROWS, D_MODEL), lambda i, l_ref: (tile_index(i), 0), **kw)

    if pipelined:
        x_specs = [act_spec(x_batch_major, lambda i: jnp.minimum(i + 1, n_steps - 1)),
                   act_spec(x_batch_major, lambda i: 0, pipeline_mode=pl.Buffered(1))]
        x_args = [x, x]
    else:
        x_specs = [act_spec(x_batch_major, lambda i: i)]
        x_args = [x]

    kernel = functools.partial(_layer_kernel, batch=batch, tile_rows=TILE_ROWS,
                               reset_first=reset_first, x_batch_major=x_batch_major,
                               y_batch_major=y_batch_major, pipelined=pipelined)
    y_shape = ((batch, n_rows // batch, D_MODEL) if y_batch_major else (n_rows, D_MODEL))
    relayout = pltpu.VMEM((N_SLABS, TILE_ROWS, LANES), f32)
    tile_f32 = pltpu.VMEM((TILE_ROWS, D_MODEL), f32)
    tile_bf16 = pltpu.VMEM((TILE_ROWS, D_MODEL), jnp.bfloat16)
    grid_spec = pltpu.PrefetchScalarGridSpec(
        num_scalar_prefetch=1,
        grid=(n_steps,),
        in_specs=x_specs + [layer_block(a) for a in weights]
                 + [layer_block(h0), layer_block(c4), layer_block(c3)],
        out_specs=[act_spec(y_batch_major, lambda i: i),
                   const_block(batch), const_block(tail4), const_block(tail3)],
        scratch_shapes=[
            pltpu.VMEM((batch, D_MODEL), f32),
            pltpu.VMEM((tail4 + TILE_ROWS, D_MODEL), f32),
            pltpu.VMEM((tail3 + TILE_ROWS, D_MODEL), f32),
            tile_f32,
            tile_f32,
            tile_f32,
            tile_f32,
            tile_f32,
            tile_bf16,
            tile_bf16,
            pltpu.VMEM((SUBLANES, D_MODEL), f32),
        ] + [relayout] * (int(x_batch_major or pipelined) + int(y_batch_major)))
    return pl.pallas_call(
        kernel,
        grid_spec=grid_spec,
        out_shape=[jax.ShapeDtypeStruct(y_shape, f32),
                   jax.ShapeDtypeStruct((batch, D_MODEL), f32),
                   jax.ShapeDtypeStruct((tail4, D_MODEL), f32),
                   jax.ShapeDtypeStruct((tail3, D_MODEL), f32)],
        compiler_params=pltpu.CompilerParams(
            dimension_semantics=("arbitrary",),
            vmem_limit_bytes=VMEM_LIMIT_BYTES),
    )(jnp.full((1,), layer, jnp.int32), *x_args, *weights, h0, c4, c3)


def _time_major(a):
    b, t, d = a.shape[-3:]
    return jnp.swapaxes(a, -3, -2).reshape(a.shape[:-3] + (t * b, d))


def _batch_major(a, batch):
    n, d = a.shape[-2:]
    return jnp.swapaxes(a.reshape(a.shape[:-2] + (n // batch, batch, d)), -3, -2)


def kernel(x_prompt, x_sample, state_rglru, state_conv4, state_conv3, w_in, b_in, conv4_w, conv4_b,
           w_rg_a, b_rg_a, w_rg_x, b_rg_x, rg_lambda, conv3_w, w_rnn_out, w_conv_out, w_out,
           ln_g, ln_b):
    bp = x_prompt.shape[0]
    bs = x_sample.shape[0]
    f32 = jnp.float32
    bf16 = jnp.bfloat16

    vec = jnp.concatenate([
        b_in.reshape(DEPTH, 8, D_MODEL), conv4_w, conv4_b[:, None], b_rg_a[:, None],
        b_rg_x[:, None], rg_lambda[:, None], conv3_w, ln_g[:, None], ln_b[:, None],
        jnp.zeros((DEPTH, V_ROWS - 21, D_MODEL), f32)], axis=1)
    weights = (w_in.astype(bf16),
               jnp.concatenate([w_rg_a, w_rg_x], axis=-1).astype(bf16),
               w_rnn_out.astype(bf16), w_conv_out.astype(bf16), w_out.astype(bf16), vec)

    yp = x_prompt
    ys = _time_major(x_sample)
    h0_p = jnp.zeros((DEPTH, bp, D_MODEL), f32)
    c4_p = jnp.zeros((DEPTH, (CONV4_W - 1) * bp, D_MODEL), f32)
    c3_p = jnp.zeros((DEPTH, (CONV3_W - 1) * bp, D_MODEL), f32)
    c4_s = _time_major(state_conv4)
    c3_s = _time_major(state_conv3)

    ph, pc4, pc3, sh, sc4, sc3 = [], [], [], [], [], []
    for l in range(DEPTH):
        yp, h, c4, c3 = _run_layer(l, yp, weights, h0_p, c4_p, c3_p, batch=bp, reset_first=True,
                                   x_batch_major=(l == 0), y_batch_major=(l == DEPTH - 1),
                                   pipelined=True)
        ph.append(h); pc4.append(c4); pc3.append(c3)
        ys, h, c4, c3 = _run_layer(l, ys, weights, state_rglru, c4_s, c3_s, batch=bs,
                                   reset_first=False)
        sh.append(h); sc4.append(c4); sc3.append(c3)

    return (yp, _batch_major(ys, bs),
            jnp.stack(ph), _batch_major(jnp.stack(pc4), bp), _batch_major(jnp.stack(pc3), bp),
            jnp.stack(sh), _batch_major(jnp.stack(sc4), bs), _batch_major(jnp.stack(sc3), bs))
```

```python
import functools

import jax
import jax.numpy as jnp
from jax import lax
from jax.experimental import pallas as pl
from jax.experimental.pallas import tpu as pltpu

D_MODEL = 1024
N_BLOCKS = 8
BLOCK = D_MODEL // N_BLOCKS
CONV4_W = 4
CONV3_W = 3
C_RG = 8.0
DEPTH = 4
ALPHA = (2.0 * DEPTH) ** 0.25
LN_EPS = 1e-5

SUBLANES = 8
LANES = 128
N_SLABS = D_MODEL // LANES
CHUNK_ROWS = 512
TILE_ROWS = 1024
OUT_SLAB_ROWS = 256
VMEM_LIMIT_BYTES = 56 * 1024 * 1024

V_BIN = 0
V_W4 = 8
V_B4 = 12
V_BA = 13
V_BX = 14
V_LAM = 15
V_W3 = 16
V_LNG = 19
V_LNB = 20
V_ROWS = 24

S_XR, S_GR, S_CB, S_CC, S_CH, S_GC, S_GRNN, S_GCONV = range(8)


def _sigmoid(x):
    return 1.0 / (1.0 + jnp.exp(-x))


def _silu(x):
    return x * _sigmoid(x)


def _lane_slab(j):
    return slice(j * LANES, (j + 1) * LANES)


def _layer_kernel(layer_ref, x_ref, w_in_ref, w_gate_ref, w_rnn_ref, w_conv_ref, w_out_ref, vec_ref,
                  h0_ref, c4_ref, c3_ref,
                  y_ref, hT_ref, c4n_ref, c3n_ref,
                  h_scr, up4_scr, up3_scr, a_scr, b_scr, sgr_scr, sgrnn_scr, mconv_scr, lhs_scr,
                  negc_scr, *relayout_scr,
                  batch, tile_rows, chunk_rows, reset_first, x_batch_major, y_batch_major):
    del layer_ref
    pid = pl.program_id(0)
    n_chunks = tile_rows // chunk_rows
    tile_steps = tile_rows // batch
    relayout_scr = list(relayout_scr)
    xtm_scr = relayout_scr.pop(0) if x_batch_major else None
    ytm_scr = relayout_scr.pop(0) if y_batch_major else None
    tail4 = (CONV4_W - 1) * batch
    tail3 = (CONV3_W - 1) * batch
    steps = chunk_rows // batch
    f32 = jnp.float32
    bf16 = jnp.bfloat16

    if x_batch_major:
        for bi in range(batch):
            for j in range(N_SLABS):
                xtm_scr[j, pl.ds(bi, tile_steps, stride=batch), :] = x_ref[bi, :, _lane_slab(j)]

    def load_x(row0, n):
        if x_batch_major:
            return jnp.concatenate([xtm_scr[j, pl.ds(row0, n), :] for j in range(N_SLABS)], axis=-1)
        return x_ref[pl.ds(row0, n), :]

    def store_y(row0, n, val):
        if y_batch_major:
            for j in range(N_SLABS):
                ytm_scr[j, pl.ds(row0, n), :] = val[:, _lane_slab(j)]
        else:
            y_ref[pl.ds(row0, n), :] = val

    @pl.when(pid == 0)
    def _init():
        h_scr[...] = h0_ref[...]
        up4_scr[0:tail4, :] = c4_ref[...]
        up3_scr[0:tail3, :] = c3_ref[...]

    nl = -vec_ref[V_LAM:V_LAM + 1, :]
    negc_scr[0:1, :] = -C_RG * (jnp.maximum(nl, 0.0) + jnp.log1p(jnp.exp(-jnp.abs(nl))))

    def vrow(r):
        return vec_ref[r:r + 1, :]

    def chunk(r0, first):
        xb = load_x(r0, chunk_rows).astype(bf16)

        def proj(k):
            w = w_in_ref[:, k * D_MODEL:(k + 1) * D_MODEL]
            return jnp.dot(xb, w, preferred_element_type=f32) + vrow(V_BIN + k)

        up4_scr[tail4:tail4 + chunk_rows, :] = proj(S_XR)
        xc = up4_scr[0:chunk_rows, :] * vrow(V_W4)
        for k in range(1, CONV4_W):
            xc = xc + up4_scr[k * batch:k * batch + chunk_rows, :] * vrow(V_W4 + k)
        b_scr[...] = xc + vrow(V_B4)
        up4_scr[0:tail4, :] = up4_scr[chunk_rows:chunk_rows + tail4, :]

        def gate_block(n):
            sl = slice(n * BLOCK, (n + 1) * BLOCK)
            xcn = b_scr[:, sl]
            g = jnp.dot(xcn.astype(bf16), w_gate_ref[n], preferred_element_type=f32)
            r = _sigmoid(g[:, :BLOCK] + vec_ref[V_BA:V_BA + 1, sl])
            i = _sigmoid(g[:, BLOCK:] + vec_ref[V_BX:V_BX + 1, sl])
            a = jnp.exp(r * negc_scr[0:1, sl])
            ix = i * xcn
            s = 1.0 - a * a
            bb = jnp.where(s > 0.0, s * lax.rsqrt(s), 0.0) * ix
            a_scr[:, sl] = a
            if first is not None:
                b_scr[0:batch, sl] = jnp.where(first, ix[0:batch], bb[0:batch])
                b_scr[batch:, sl] = bb[batch:]
            else:
                b_scr[:, sl] = bb

        def conv_cc():
            up3_scr[tail3:tail3 + chunk_rows, :] = proj(S_CC)

        def conv_ch():
            up3_scr[tail3:tail3 + chunk_rows, :] = up3_scr[tail3:tail3 + chunk_rows, :] * proj(S_CH)

        def conv_gc():
            mconv_scr[...] = _silu(proj(S_GC))

        def conv_cb():
            v = up3_scr[0:chunk_rows, :] * vrow(V_W3)
            for k in range(1, CONV3_W):
                v = v + up3_scr[k * batch:k * batch + chunk_rows, :] * vrow(V_W3 + k)
            lhs_scr[...] = (proj(S_CB) * v * mconv_scr[...]).astype(bf16)
            up3_scr[0:tail3, :] = up3_scr[chunk_rows:chunk_rows + tail3, :]

        def conv_out():
            mconv_scr[...] = jnp.dot(lhs_scr[...], w_conv_ref[...], preferred_element_type=f32)

        def conv_merge():
            mconv_scr[...] = _sigmoid(proj(S_GCONV)) * mconv_scr[...]

        def rnn_gr():
            sgr_scr[...] = _silu(proj(S_GR))

        def rnn_grnn():
            sgrnn_scr[...] = _sigmoid(proj(S_GRNN))

        between = [conv_cc, conv_ch, conv_gc, conv_cb, conv_out, conv_merge, rnn_gr, rnn_grnn]
        for n in range(N_BLOCKS):
            gate_block(n)
            between[n]()

        for gidx in range(batch // SUBLANES):
            g0 = gidx * SUBLANES
            h = h_scr[g0:g0 + SUBLANES, :]
            for t in range(steps):
                row = t * batch + g0
                h = a_scr[row:row + SUBLANES, :] * h + b_scr[row:row + SUBLANES, :]
                b_scr[row:row + SUBLANES, :] = h
            h_scr[g0:g0 + SUBLANES, :] = h
        lhs_scr[...] = (b_scr[...] * sgr_scr[...]).astype(bf16)

        slabs = [pl.ds(s0, OUT_SLAB_ROWS) for s0 in range(0, chunk_rows, OUT_SLAB_ROWS)]
        y_rnn = [jnp.dot(lhs_scr[sl, :], w_rnn_ref[...], preferred_element_type=f32)
                 for sl in slabs]
        for s0, sl, yr in zip(range(0, chunk_rows, OUT_SLAB_ROWS), slabs, y_rnn):
            m = sgrnn_scr[sl, :] * yr + mconv_scr[sl, :]
            out = jnp.dot(m.astype(bf16), w_out_ref[...], preferred_element_type=f32)
            yres = ALPHA * load_x(r0 + s0, OUT_SLAB_ROWS) + out
            mu = jnp.mean(yres, axis=-1, keepdims=True)
            d = yres - mu
            var = jnp.mean(d * d, axis=-1, keepdims=True)
            store_y(r0 + s0, OUT_SLAB_ROWS,
                    d * lax.rsqrt(var + LN_EPS) * vrow(V_LNG) + vrow(V_LNB))

    if n_chunks == 1:
        chunk(0, (pid == 0) if reset_first else None)
    else:
        def chunk_body(c, carry):
            first = jnp.logical_and(pid == 0, c == 0) if reset_first else None
            chunk(pl.multiple_of(c * chunk_rows, chunk_rows), first)
            return carry
        lax.fori_loop(0, n_chunks, chunk_body, 0)

    if y_batch_major:
        for bi in range(batch):
            for j in range(N_SLABS):
                y_ref[bi, :, _lane_slab(j)] = ytm_scr[j, pl.ds(bi, tile_steps, stride=batch), :]

    hT_ref[...] = h_scr[...]
    c4n_ref[...] = up4_scr[0:tail4, :]
    c3n_ref[...] = up3_scr[0:tail3, :]


def _run_layer(layer, x, weights, h0, c4, c3, *, batch, reset_first,
               x_batch_major=False, y_batch_major=False):
    n_rows = x.shape[0] * x.shape[1] if x_batch_major else x.shape[0]
    assert n_rows % TILE_ROWS == 0 and TILE_ROWS % CHUNK_ROWS == 0
    assert CHUNK_ROWS % batch == 0 and batch % SUBLANES == 0
    assert CHUNK_ROWS % OUT_SLAB_ROWS == 0
    tile_steps = TILE_ROWS // batch
    tail4 = (CONV4_W - 1) * batch
    tail3 = (CONV3_W - 1) * batch
    f32 = jnp.float32

    def layer_block(a):
        nd = a.ndim
        return pl.BlockSpec((None,) + a.shape[1:],
                            lambda i, l_ref, _nd=nd: (l_ref[0],) + (0,) * (_nd - 1),
                            pipeline_mode=pl.Buffered(1))

    def const_block(rows):
        return pl.BlockSpec((rows, D_MODEL), lambda i, l_ref: (0, 0))

    kernel = functools.partial(_layer_kernel, batch=batch, tile_rows=TILE_ROWS,
                               chunk_rows=CHUNK_ROWS, reset_first=reset_first,
                               x_batch_major=x_batch_major, y_batch_major=y_batch_major)
    row_spec = pl.BlockSpec((TILE_ROWS, D_MODEL), lambda i, l_ref: (i, 0))
    bm_spec = pl.BlockSpec((batch, tile_steps, D_MODEL), lambda i, l_ref: (0, i, 0))
    bm_shape = jax.ShapeDtypeStruct((batch, n_rows // batch, D_MODEL), f32)
    relayout = pltpu.VMEM((N_SLABS, TILE_ROWS, LANES), f32)
    chunk_f32 = pltpu.VMEM((CHUNK_ROWS, D_MODEL), f32)
    grid_spec = pltpu.PrefetchScalarGridSpec(
        num_scalar_prefetch=1,
        grid=(n_rows // TILE_ROWS,),
        in_specs=[bm_spec if x_batch_major else row_spec] + [layer_block(a) for a in weights]
                 + [layer_block(h0), layer_block(c4), layer_block(c3)],
        out_specs=[bm_spec if y_batch_major else row_spec,
                   const_block(batch), const_block(tail4), const_block(tail3)],
        scratch_shapes=[
            pltpu.VMEM((batch, D_MODEL), f32),
            pltpu.VMEM((tail4 + CHUNK_ROWS, D_MODEL), f32),
            pltpu.VMEM((tail3 + CHUNK_ROWS, D_MODEL), f32),
            chunk_f32,
            chunk_f32,
            chunk_f32,
            chunk_f32,
            chunk_f32,
            pltpu.VMEM((CHUNK_ROWS, D_MODEL), jnp.bfloat16),
            pltpu.VMEM((SUBLANES, D_MODEL), f32),
        ] + [relayout] * (int(x_batch_major) + int(y_batch_major)))
    return pl.pallas_call(
        kernel,
        grid_spec=grid_spec,
        out_shape=[bm_shape if y_batch_major else jax.ShapeDtypeStruct((n_rows, D_MODEL), f32),
                   jax.ShapeDtypeStruct((batch, D_MODEL), f32),
                   jax.ShapeDtypeStruct((tail4, D_MODEL), f32),
                   jax.ShapeDtypeStruct((tail3, D_MODEL), f32)],
        compiler_params=pltpu.CompilerParams(
            dimension_semantics=("arbitrary",),
            vmem_limit_bytes=VMEM_LIMIT_BYTES),
    )(jnp.full((1,), layer, jnp.int32), x, *weights, h0, c4, c3)


def _time_major(a):
    b, t, d = a.shape[-3:]
    return jnp.swapaxes(a, -3, -2).reshape(a.shape[:-3] + (t * b, d))


def _batch_major(a, batch):
    n, d = a.shape[-2:]
    return jnp.swapaxes(a.reshape(a.shape[:-2] + (n // batch, batch, d)), -3, -2)


def kernel(x_prompt, x_sample, state_rglru, state_conv4, state_conv3, w_in, b_in, conv4_w, conv4_b,
           w_rg_a, b_rg_a, w_rg_x, b_rg_x, rg_lambda, conv3_w, w_rnn_out, w_conv_out, w_out,
           ln_g, ln_b):
    bp = x_prompt.shape[0]
    bs = x_sample.shape[0]
    f32 = jnp.float32
    bf16 = jnp.bfloat16

    vec = jnp.concatenate([
        b_in.reshape(DEPTH, 8, D_MODEL), conv4_w, conv4_b[:, None], b_rg_a[:, None],
        b_rg_x[:, None], rg_lambda[:, None], conv3_w, ln_g[:, None], ln_b[:, None],
        jnp.zeros((DEPTH, V_ROWS - 21, D_MODEL), f32)], axis=1)
    weights = (w_in.astype(bf16),
               jnp.concatenate([w_rg_a, w_rg_x], axis=-1).astype(bf16),
               w_rnn_out.astype(bf16), w_conv_out.astype(bf16), w_out.astype(bf16), vec)

    yp = x_prompt
    ys = _time_major(x_sample)
    h0_p = jnp.zeros((DEPTH, bp, D_MODEL), f32)
    c4_p = jnp.zeros((DEPTH, (CONV4_W - 1) * bp, D_MODEL), f32)
    c3_p = jnp.zeros((DEPTH, (CONV3_W - 1) * bp, D_MODEL), f32)
    c4_s = _time_major(state_conv4)
    c3_s = _time_major(state_conv3)

    ph, pc4, pc3, sh, sc4, sc3 = [], [], [], [], [], []
    for l in range(DEPTH):
        yp, h, c4, c3 = _run_layer(l, yp, weights, h0_p, c4_p, c3_p, batch=bp, reset_first=True,
                                   x_batch_major=(l == 0), y_batch_major=(l == DEPTH - 1))
        ph.append(h); pc4.append(c4); pc3.append(c3)
        ys, h, c4, c3 = _run_layer(l, ys, weights, state_rglru, c4_s, c3_s, batch=bs,
                                   reset_first=False)
        sh.append(h); sc4.append(c4); sc3.append(c3)

    return (yp, _batch_major(ys, bs),
            jnp.stack(ph), _batch_major(jnp.stack(pc4), bp), _batch_major(jnp.stack(pc3), bp),
            jnp.stack(sh), _batch_major(jnp.stack(sc4), bs), _batch_major(jnp.stack(sc3), bs))
```
